```python
import math
import jax, jax.numpy as jnp
from jax import lax
import numpy as np

D_MODEL = 1024
BATCH = 2
SEQ = 8192
DEPTH = 1
DEC_BATCH = 128
DEC_SEQ = 1
PAST_LEN = 2048
PAGE_SIZE = 128

H_A = 4
DK_A = 64
DV_A = 2 * DK_A
H_R = 4
DK_R = 64
DV_R = 128
RET_CHUNK = 128
N_MEM = 256
H_M = 4
DH_M = D_MODEL // H_M
D_FF = 4 * D_MODEL
NUM_BUCKETS = 32
MAX_DISTANCE = 128
Q_BLOCK = 128
EPS = 1e-6
ROPE_BASE = 10000.0
WA_QK = H_A * 2 * DK_A
WA_V = H_A * DV_A
WR_QK = H_R * DK_R
WR_V = H_R * DV_R
D_IN = 2 * WA_QK + WA_V + 2 * WR_QK + 2 * WR_V
D_MIX = H_A * DV_A + H_R * DV_R

kernel_name = 'hybrid_diffattn_retention_decoder_step'


def rmsnorm(x, g):
    xf = x.astype(jnp.float32)
    y = xf * lax.rsqrt(jnp.mean(xf * xf, axis=-1, keepdims=True) + EPS)
    return (y * g.astype(jnp.float32)).astype(x.dtype)


def head_rms(x):
    return x * lax.rsqrt(jnp.mean(x * x, axis=-1, keepdims=True) + EPS)


def t5_bucket(rel):
    n = jnp.maximum(rel, 0)
    max_exact = NUM_BUCKETS // 2
    nf = jnp.maximum(n, 1).astype(jnp.float32)
    large = max_exact + (jnp.log(nf / max_exact) / math.log(MAX_DISTANCE / max_exact)
                         * (NUM_BUCKETS - max_exact)).astype(jnp.int32)
    large = jnp.minimum(large, NUM_BUCKETS - 1)
    return jnp.where(n < max_exact, n, large)


def project(h, w_in):
    B, T = h.shape[0], h.shape[1]
    z = h @ w_in
    sizes = [WA_QK, WA_QK, WA_V, WR_QK, WR_QK, WR_V, WR_V]
    offs = np.cumsum(sizes)[:-1].tolist()
    qa, ka, va, qr, kr, vr, gr = jnp.split(z, offs, axis=-1)
    return (qa.reshape(B, T, H_A, 2 * DK_A), ka.reshape(B, T, H_A, 2 * DK_A),
            va.reshape(B, T, H_A, DV_A), qr.reshape(B, T, H_R, DK_R),
            kr.reshape(B, T, H_R, DK_R), vr.reshape(B, T, H_R, DV_R), gr)


def diff_attention(q, k, v, q_pos, k_pos, rel_bias, lam):
    f32 = jnp.float32
    scale = DK_A ** -0.5
    q1, q2 = q[..., :DK_A], q[..., DK_A:]
    k1, k2 = k[..., :DK_A], k[..., DK_A:]
    rel = q_pos[:, None] - k_pos[None, :]
    bias = jnp.transpose(rel_bias[t5_bucket(rel)], (2, 0, 1))[None].astype(f32)
    mask = (rel >= 0)[None, None]
    neg = jnp.finfo(f32).min

    def probs(qi, ki):
        s = jnp.einsum('bqhd,bkhd->bhqk', qi, ki).astype(f32) * scale + bias
        return jax.nn.softmax(jnp.where(mask, s, neg), axis=-1)

    attn = probs(q1, k1) - lam * probs(q2, k2)
    return jnp.einsum('bhqk,bkhd->bqhd', attn, v.astype(f32))


def diff_attention_prompt(qa, ka, va, rel_bias, lam):
    B, T = qa.shape[0], qa.shape[1]
    nb = T // Q_BLOCK
    k_pos = jnp.arange(T)

    def block(i):
        qb = lax.dynamic_slice_in_dim(qa, i * Q_BLOCK, Q_BLOCK, axis=1)
        q_pos = i * Q_BLOCK + jnp.arange(Q_BLOCK)
        return diff_attention(qb, ka, va, q_pos, k_pos, rel_bias, lam)

    ob = lax.map(block, jnp.arange(nb))
    return jnp.moveaxis(ob, 0, 1).reshape(B, T, H_A, DV_A)


def rotary(x, pos):
    half = x.shape[-1] // 2
    inv_freq = ROPE_BASE ** (-jnp.linspace(0.0, 1.0, half, dtype=jnp.float32))
    ang = pos.astype(jnp.float32)[:, None] * inv_freq[None, :]
    cos = jnp.cos(ang)[:, None, :]
    sin = jnp.sin(ang)[:, None, :]
    xf = x.astype(jnp.float32)
    x1, x2 = xf[..., :half], xf[..., half:]
    return jnp.concatenate([x1 * cos - x2 * sin, x2 * cos + x1 * sin], axis=-1)


def retention_log_decay():
    return jnp.log(1.0 - 2.0 ** (-5.0 - jnp.arange(H_R, dtype=jnp.float32)))


def retention_chunk(S, q, k, v, lg):
    C = q.shape[1]
    i = jnp.arange(C, dtype=jnp.float32)
    d = i[:, None] - i[None, :]
    decay = jnp.where(d >= 0, jnp.exp(d[None] * lg[:, None, None]), 0.0)
    scores = jnp.einsum('bihd,bjhd->bhij', q, k) * decay[None]
    o = jnp.einsum('bhij,bjhe->bihe', scores, v)
    o = o + jnp.einsum('bihd,bhde->bihe', q, S) * jnp.exp((i + 1.0)[:, None] * lg[None, :])[None, :, :, None]
    k_dec = k * jnp.exp((C - 1.0 - i)[:, None] * lg[None, :])[None, :, :, None]
    S_new = jnp.exp(C * lg)[None, :, None, None] * S + jnp.einsum('bjhd,bjhe->bhde', k_dec, v)
    return S_new, o


def retention_prompt(q, k, v, lg):
    B, T = q.shape[0], q.shape[1]
    nc = T // RET_CHUNK

    def chunks(a):
        return jnp.moveaxis(a.reshape(B, nc, RET_CHUNK, a.shape[2], a.shape[3]), 1, 0)

    S0 = jnp.zeros((B, H_R, DK_R, DV_R), jnp.float32)
    S, oc = lax.scan(lambda s, xs: retention_chunk(s, xs[0], xs[1], xs[2], lg),
                     S0, (chunks(q), chunks(k), chunks(v)))
    return S, jnp.moveaxis(oc, 0, 1).reshape(B, T, H_R, DV_R)


def merge_groups(oa, orr, gr, subln, lam_init):
    B, T = oa.shape[0], oa.shape[1]
    a = head_rms(oa) * subln.astype(jnp.float32) * (1.0 - lam_init)
    r = head_rms(orr).reshape(B, T, H_R * DV_R) * jax.nn.silu(gr.astype(jnp.float32))
    return jnp.concatenate([a.reshape(B, T, H_A * DV_A), r], axis=-1)


def mem_kv(mem, g, w_mk, w_mv):
    B, M = mem.shape[0], mem.shape[1]
    h = rmsnorm(mem, g)
    return (h @ w_mk).reshape(B, M, H_M, DH_M), (h @ w_mv).reshape(B, M, H_M, DH_M)


def mem_attend(h, mk, mv, w_mq, w_mo):
    B, T = h.shape[0], h.shape[1]
    q = (h @ w_mq).reshape(B, T, H_M, DH_M)
    s = jnp.einsum('bthd,bmhd->bhtm', q, mk.astype(q.dtype)).astype(jnp.float32) * DH_M ** -0.5
    p = jax.nn.softmax(s, axis=-1)
    o = jnp.einsum('bhtm,bmhd->bthd', p, mv.astype(jnp.float32)).reshape(B, T, D_MODEL)
    return o.astype(h.dtype) @ w_mo


def sq_relu_mlp(h, w_up, w_down):
    return jnp.square(jax.nn.relu(h @ w_up)) @ w_down


def setup_inputs(seed: int = 0) -> dict:
    key = jax.random.key(seed)
    ks = jax.random.split(key, 32)
    f32 = jnp.float32
    n_pages = PAST_LEN // PAGE_SIZE
    n_used = DEC_BATCH * n_pages
    n_phys = n_used + max(n_used // 4, 1)

    def nrm(k, shape, s=1.0):
        return jax.random.normal(k, shape, f32) * s

    def gain(k, shape):
        return 1.0 + 0.05 * jax.random.normal(k, shape, f32)

    page_table = jax.random.permutation(ks[5], n_phys)[:n_used].reshape(DEC_BATCH, n_pages).astype(jnp.int32)
    return {
        'x_prompt': nrm(ks[0], (BATCH, SEQ, D_MODEL)),
        'x_sample': nrm(ks[1], (DEC_BATCH, DEC_SEQ, D_MODEL)),
        'mem_prompt': nrm(ks[2], (BATCH, N_MEM, D_MODEL)),
        'cache_k': nrm(ks[3], (DEPTH, n_phys, PAGE_SIZE, H_A, 2 * DK_A)),
        'cache_v': nrm(ks[4], (DEPTH, n_phys, PAGE_SIZE, H_A, DV_A)),
        'page_table': page_table,
        'state_ret': nrm(ks[6], (DEPTH, DEC_BATCH, H_R, DK_R, DV_R), 0.5),
        'cache_mem_k': nrm(ks[7], (DEPTH, DEC_BATCH, N_MEM, H_M, DH_M)),
        'cache_mem_v': nrm(ks[8], (DEPTH, DEC_BATCH, N_MEM, H_M, DH_M)),
        'rel_bias': nrm(ks[9], (NUM_BUCKETS, H_A), 0.5),
        'norm_mix': gain(ks[10], (DEPTH, D_MODEL)),
        'w_in': nrm(ks[11], (DEPTH, D_MODEL, D_IN), D_MODEL ** -0.5),
        'lambda_q1': nrm(ks[12], (DEPTH, DK_A), 0.1),
        'lambda_k1': nrm(ks[13], (DEPTH, DK_A), 0.1),
        'lambda_q2': nrm(ks[14], (DEPTH, DK_A), 0.1),
        'lambda_k2': nrm(ks[15], (DEPTH, DK_A), 0.1),
        'subln_a': gain(ks[16], (DEPTH, DV_A)),
        'w_out': nrm(ks[17], (DEPTH, D_MIX, D_MODEL), D_MIX ** -0.5),
        'norm_mem_q': gain(ks[18], (DEPTH, D_MODEL)),
        'norm_mem_kv': gain(ks[19], (DEPTH, D_MODEL)),
        'w_mq': nrm(ks[20], (DEPTH, D_MODEL, D_MODEL), D_MODEL ** -0.5),
        'w_mk': nrm(ks[21], (DEPTH, D_MODEL, D_MODEL), D_MODEL ** -0.5),
        'w_mv': nrm(ks[22], (DEPTH, D_MODEL, D_MODEL), D_MODEL ** -0.5),
        'w_mo': nrm(ks[23], (DEPTH, D_MODEL, D_MODEL), D_MODEL ** -0.5),
        'norm_mlp': gain(ks[24], (DEPTH, D_MODEL)),
        'w_up': nrm(ks[25], (DEPTH, D_MODEL, D_FF), D_MODEL ** -0.5),
        'w_down': nrm(ks[26], (DEPTH, D_FF, D_MODEL), D_FF ** -0.5),
        'norm_final': gain(ks[27], (D_MODEL,)),
    }


def reference(x_prompt, x_sample, mem_prompt, cache_k, cache_v, page_table, state_ret,
              cache_mem_k, cache_mem_v, rel_bias, norm_mix, w_in, lambda_q1, lambda_k1,
              lambda_q2, lambda_k2, subln_a, w_out, norm_mem_q, norm_mem_kv, w_mq, w_mk,
              w_mv, w_mo, norm_mlp, w_up, w_down, norm_final):
    f32 = jnp.float32
    B, T = x_prompt.shape[0], x_prompt.shape[1]
    Bd, Td = x_sample.shape[0], x_sample.shape[1]
    past = page_table.shape[1] * PAGE_SIZE
    pos_p = jnp.arange(T)
    pos_d = past + jnp.arange(Td)
    kpos_d = jnp.arange(past + Td)
    lg = retention_log_decay()
    kr_scale = DK_R ** -0.5

    xp, xd = x_prompt, x_sample
    kp_l, vp_l, sp_l, mkp_l, mvp_l, kd_l, vd_l, sd_l = [], [], [], [], [], [], [], []
    for l in range(DEPTH):
        lam_init = 0.8 - 0.6 * math.exp(-0.3 * l)
        lam = (jnp.exp(jnp.sum(lambda_q1[l].astype(f32) * lambda_k1[l].astype(f32)))
               - jnp.exp(jnp.sum(lambda_q2[l].astype(f32) * lambda_k2[l].astype(f32))) + lam_init)

        hp = rmsnorm(xp, norm_mix[l])
        qa, ka, va, qr, kr, vr, gr = project(hp, w_in[l])
        oa = diff_attention_prompt(qa, ka, va, rel_bias, lam)
        s_p, o_r = retention_prompt(rotary(qr, pos_p), rotary(kr, pos_p) * kr_scale, vr.astype(f32), lg)
        xp = xp + merge_groups(oa, o_r, gr, subln_a[l], lam_init).astype(xp.dtype) @ w_out[l]
        mk_p, mv_p = mem_kv(mem_prompt, norm_mem_kv[l], w_mk[l], w_mv[l])
        xp = xp + mem_attend(rmsnorm(xp, norm_mem_q[l]), mk_p, mv_p, w_mq[l], w_mo[l])
        xp = xp + sq_relu_mlp(rmsnorm(xp, norm_mlp[l]), w_up[l], w_down[l])
        kp_l.append(ka.reshape(B, T // PAGE_SIZE, PAGE_SIZE, H_A, 2 * DK_A))
        vp_l.append(va.reshape(B, T // PAGE_SIZE, PAGE_SIZE, H_A, DV_A))
        sp_l.append(s_p)
        mkp_l.append(mk_p)
        mvp_l.append(mv_p)

        hd = rmsnorm(xd, norm_mix[l])
        qa_d, ka_d, va_d, qr_d, kr_d, vr_d, gr_d = project(hd, w_in[l])
        k_past = cache_k[l][page_table].reshape(Bd, past, H_A, 2 * DK_A)
        v_past = cache_v[l][page_table].reshape(Bd, past, H_A, DV_A)
        k_all = jnp.concatenate([k_past.astype(ka_d.dtype), ka_d], axis=1)
        v_all = jnp.concatenate([v_past.astype(va_d.dtype), va_d], axis=1)
        oa_d = diff_attention(qa_d, k_all, v_all, pos_d, kpos_d, rel_bias, lam)
        s_d, o_rd = retention_chunk(state_ret[l].astype(f32), rotary(qr_d, pos_d),
                                    rotary(kr_d, pos_d) * kr_scale, vr_d.astype(f32), lg)
        xd = xd + merge_groups(oa_d, o_rd, gr_d, subln_a[l], lam_init).astype(xd.dtype) @ w_out[l]
        xd = xd + mem_attend(rmsnorm(xd, norm_mem_q[l]), cache_mem_k[l], cache_mem_v[l], w_mq[l], w_mo[l])
        xd = xd + sq_relu_mlp(rmsnorm(xd, norm_mlp[l]), w_up[l], w_down[l])
        kd_l.append(ka_d)
        vd_l.append(va_d)
        sd_l.append(s_d)

    y_prompt = rmsnorm(xp, norm_final)
    y_sample = rmsnorm(xd, norm_final)
    k_prompt = jnp.stack(kp_l)
    v_prompt = jnp.stack(vp_l)
    state_ret_prompt = jnp.stack(sp_l)
    mem_k_prompt = jnp.stack(mkp_l)
    mem_v_prompt = jnp.stack(mvp_l)
    k_sample = jnp.stack(kd_l)
    v_sample = jnp.stack(vd_l)
    state_ret_sample = jnp.stack(sd_l)
    return (y_prompt, y_sample, k_prompt, v_prompt, state_ret_prompt, mem_k_prompt, mem_v_prompt, k_sample, v_sample, state_ret_sample)
```

```python
import functools
import math

import numpy as np
import jax
import jax.numpy as jnp
from jax import lax
from jax.experimental import pallas as pl
from jax.experimental.pallas import tpu as pltpu

_F32 = jnp.float32
_BF16 = jnp.bfloat16

D_MODEL = 1024
PAGE_SIZE = 128
H_A, DK_A, DV_A = 4, 64, 128
H_R, DK_R, DV_R = 4, 64, 128
N_MEM, H_M = 256, 4
DH_M = D_MODEL // H_M
D_FF = 4 * D_MODEL
NUM_BUCKETS, MAX_DISTANCE = 32, 128
EPS = 1e-6
ROPE_BASE = 10000.0
WA_QK, WA_V = H_A * 2 * DK_A, H_A * DV_A
WR_QK, WR_V = H_R * DK_R, H_R * DV_R
D_IN = 2 * WA_QK + WA_V + 2 * WR_QK + 2 * WR_V
NEG_BIG = -1e30

VMEM_LIMIT_BYTES = 56 * 1024 * 1024

TM_PROJ = 512
TQ_ATTN = 256
RET_CHUNK = 128
FF_CHUNK = 1024
BB_RET = 8


def _dot(a, b):
    return jnp.dot(a, b, preferred_element_type=_F32)


def _dot_nt(a, b):
    return lax.dot_general(a, b, (((1,), (1,)), ((), ())), preferred_element_type=_F32)


def _rmsnorm(x, g):
    return x * lax.rsqrt(jnp.mean(x * x, axis=-1, keepdims=True) + EPS) * g


def _head_rms(x):
    return x * lax.rsqrt(jnp.mean(x * x, axis=-1, keepdims=True) + EPS)


def _params(*sem):
    return pltpu.CompilerParams(dimension_semantics=sem, vmem_limit_bytes=VMEM_LIMIT_BYTES)


def _const_spec(shape):
    nd = len(shape)
    return pl.BlockSpec(shape, lambda *_: (0,) * nd, pipeline_mode=pl.Buffered(1))


def _rotary(x, cos, sin_signed):
    n = x.shape[-1]
    half = DK_R // 2
    lane = lax.broadcasted_iota(jnp.int32, x.shape, 1)
    up = pltpu.roll(x, n - half, 1)
    down = pltpu.roll(x, half, 1)
    partner = jnp.where((lane & half) == 0, up, down)
    return x * cos + partner * sin_signed


def _inproj_body(x_ref, g_ref, w_ref, cos_ref, sin_ref,
                 qa_ref, kaf_ref, kab_ref, vaf_ref, vab_ref, qr_ref, kr_ref, vr_ref, gr_ref):
    h = _rmsnorm(x_ref[...], g_ref[...]).astype(_BF16)

    def proj(lo, n):
        return _dot(h, w_ref[:, lo:lo + n])

    o = 0
    qa_ref[...] = proj(o, WA_QK).astype(_BF16)
    o += WA_QK
    ka = proj(o, WA_QK)
    kaf_ref[...] = ka
    kab_ref[...] = ka.astype(_BF16)
    o += WA_QK
    va = proj(o, WA_V)
    vaf_ref[...] = va
    vab_ref[...] = va.astype(_BF16)
    o += WA_V
    cos, sin = cos_ref[...], sin_ref[...]
    qr_ref[...] = _rotary(proj(o, WR_QK), cos, sin)
    o += WR_QK
    kr_ref[...] = _rotary(proj(o, WR_QK), cos, sin) * (DK_R ** -0.5)
    o += WR_QK
    vr_ref[...] = proj(o, WR_V).astype(_BF16)
    o += WR_V
    gr_ref[...] = proj(o, WR_V)


def _inproj(x2d, g, w_bf, cos, sin, tm):
    m = x2d.shape[0]
    npos = cos.shape[0] // tm
    row = lambda i: (i, 0)
    pos = lambda i: (i % npos, 0)

    def out(n, dt):
        return jax.ShapeDtypeStruct((m, n), dt), pl.BlockSpec((tm, n), row)

    outs = [out(WA_QK, _BF16), out(WA_QK, _F32), out(WA_QK, _BF16), out(WA_V, _F32), out(WA_V, _BF16),
            out(WR_QK, _F32), out(WR_QK, _F32), out(WR_V, _BF16), out(WR_V, _F32)]
    return pl.pallas_call(
        _inproj_body,
        grid=(m // tm,),
        in_specs=[pl.BlockSpec((tm, D_MODEL), row), _const_spec((1, D_MODEL)), _const_spec((D_MODEL, D_IN)),
                  pl.BlockSpec((tm, WR_QK), pos), pl.BlockSpec((tm, WR_QK), pos)],
        out_specs=[o[1] for o in outs],
        out_shape=[o[0] for o in outs],
        compiler_params=_params("parallel"),
        name="inproj",
    )(x2d, g, w_bf, cos, sin)


def _rotary_tables(pos):
    half = DK_R // 2
    inv_freq = ROPE_BASE ** (-jnp.linspace(0.0, 1.0, half, dtype=_F32))
    ang = pos.astype(_F32)[:, None] * inv_freq[None, :]
    cos, sin = jnp.cos(ang), jnp.sin(ang)
    return (jnp.tile(cos, (1, 2 * H_R)), jnp.tile(jnp.concatenate([-sin, sin], axis=-1), (1, H_R)))


def _t5_bucket_np(rel):
    n = np.maximum(rel, 0)
    max_exact = NUM_BUCKETS // 2
    nf = np.maximum(n, 1).astype(np.float32)
    large = max_exact + (np.log(nf / np.float32(max_exact)) / np.float32(math.log(MAX_DISTANCE / max_exact))
                         * np.float32(NUM_BUCKETS - max_exact)).astype(np.int32)
    large = np.minimum(large, NUM_BUCKETS - 1)
    return np.where(n < max_exact, n, large).astype(np.int32)


def _lambda_value(lq1_ref, lk1_ref, lq2_ref, lk2_ref, lam_init):
    a = jnp.sum(lq1_ref[...] * lk1_ref[...], axis=-1, keepdims=True)
    b = jnp.sum(lq2_ref[...] * lk2_ref[...], axis=-1, keepdims=True)
    return jnp.exp(a) - jnp.exp(b) + lam_init


def _dattn_body(relt_ref, lq1_ref, lk1_ref, lq2_ref, lk2_ref, subln_ref, q_ref, k_ref, v_ref, bias_ref,
                o_ref, qs_ref, m_ref, l_ref, acc_ref, *, tq, lam_init):
    h = pl.program_id(1)
    qi = pl.program_id(2)
    q = q_ref[...].astype(_F32) * (DK_A ** -0.5)
    lane = lax.broadcasted_iota(jnp.int32, q.shape, 1)
    qs_ref[0:tq, :] = jnp.where(lane < DK_A, q, 0.0).astype(_BF16)
    qs_ref[tq:, :] = jnp.where(lane >= DK_A, q, 0.0).astype(_BF16)
    m_ref[...] = jnp.full(m_ref.shape, NEG_BIG, _F32)
    l_ref[...] = jnp.zeros(l_ref.shape, _F32)
    acc_ref[...] = jnp.zeros(acc_ref.shape, _F32)

    def block(start, bias):
        kb = k_ref[pl.ds(start, tq), :]
        vb = v_ref[pl.ds(start, tq), :]
        s = _dot_nt(qs_ref[...], kb) + bias
        m_old = m_ref[...]
        m_new = jnp.maximum(m_old, jnp.max(s, axis=-1, keepdims=True))
        alpha = jnp.exp(m_old - m_new)
        p = jnp.exp(s - m_new)
        l_ref[...] = alpha * l_ref[...] + jnp.sum(p, axis=-1, keepdims=True)
        acc_ref[...] = alpha * acc_ref[...] + _dot(p.astype(_BF16), vb)
        m_ref[...] = m_new

    far_bias = relt_ref[h, NUM_BUCKETS - 1]

    def far(ki, carry):
        block(pl.multiple_of(ki * tq, tq), far_bias)
        return carry

    lax.fori_loop(0, jnp.maximum(qi - 1, 0), far, 0)

    @pl.when(qi >= 1)
    def _():
        b1 = bias_ref[1]
        block(pl.multiple_of((qi - 1) * tq, tq), jnp.concatenate([b1, b1], axis=0))

    b0 = bias_ref[0]
    block(pl.multiple_of(qi * tq, tq), jnp.concatenate([b0, b0], axis=0))

    lam = _lambda_value(lq1_ref, lk1_ref, lq2_ref, lk2_ref, lam_init)
    acc = acc_ref[...]
    l = l_ref[...]
    o = acc[:tq] / l[:tq] - lam * (acc[tq:] / l[tq:])
    o_ref[...] = (_head_rms(o) * subln_ref[...] * (1.0 - lam_init)).astype(_BF16)


def _dattn_prompt(qa, kab, vab, rel_bias, lams, subln, batch, seq, lam_init):
    tq = TQ_ATTN
    nq = seq // tq
    i = np.arange(tq)
    rel0 = i[:, None] - i[None, :]
    bucket = np.stack([_t5_bucket_np(rel0), _t5_bucket_np(rel0 + tq)])
    assert int(_t5_bucket_np(np.array([tq + 1]))[0]) == NUM_BUCKETS - 1
    bias = jnp.transpose(rel_bias[bucket], (3, 0, 1, 2)).astype(_F32)
    mask = jnp.asarray(np.stack([rel0 >= 0, np.ones_like(rel0, bool)]))
    bias = jnp.where(mask[None], bias, NEG_BIG)
    vec = pl.BlockSpec((1, DK_A), lambda b, h, i: (0, 0))
    return pl.pallas_call(
        functools.partial(_dattn_body, tq=tq, lam_init=lam_init),
        grid=(batch, H_A, nq),
        in_specs=[pl.BlockSpec(memory_space=pltpu.SMEM), vec, vec, vec, vec,
                  pl.BlockSpec((1, DV_A), lambda b, h, i: (0, 0)),
                  pl.BlockSpec((tq, 2 * DK_A), lambda b, h, i: (b * nq + i, h)),
                  pl.BlockSpec((seq, 2 * DK_A), lambda b, h, i: (b, h)),
                  pl.BlockSpec((seq, DV_A), lambda b, h, i: (b, h)),
                  pl.BlockSpec((None, 2, tq, tq), lambda b, h, i: (h, 0, 0, 0))],
        out_specs=pl.BlockSpec((tq, DV_A), lambda b, h, i: (b * nq + i, h)),
        out_shape=jax.ShapeDtypeStruct((batch * seq, H_A * DV_A), _BF16),
        scratch_shapes=[pltpu.VMEM((2 * tq, 2 * DK_A), _BF16), pltpu.VMEM((2 * tq, 1), _F32),
                        pltpu.VMEM((2 * tq, 1), _F32), pltpu.VMEM((2 * tq, DV_A), _F32)],
        compiler_params=_params("parallel", "parallel", "parallel"),
        name="dattn_prompt",
    )(rel_bias.T.astype(_F32), *lams, subln, qa, kab, vab, bias)


def _retention_log_decay():
    return jnp.log(1.0 - 2.0 ** (-5.0 - jnp.arange(H_R, dtype=_F32)))


def _ret_body(q_ref, k_ref, v_ref, g_ref, dmat_ref, rowdec_ref, kdec_ref, sdec_ref, r_ref, s_out_ref, s_ref):
    c = pl.program_id(1)

    @pl.when(c == 0)
    def _():
        s_ref[...] = jnp.zeros(s_ref.shape, _F32)

    for h in range(H_R):
        qh = q_ref[:, h * DK_R:(h + 1) * DK_R].astype(_BF16)
        kh = k_ref[:, h * DK_R:(h + 1) * DK_R]
        vh = v_ref[:, h * DV_R:(h + 1) * DV_R]
        state = s_ref[h]
        scores = _dot_nt(qh, kh.astype(_BF16)) * dmat_ref[h]
        o = _dot(scores.astype(_BF16), vh) + _dot(qh, state.astype(_BF16)) * rowdec_ref[h]
        k_dec_t = (kh * kdec_ref[h]).T.astype(_BF16)
        s_ref[h] = sdec_ref[h] * state + _dot(k_dec_t, vh)
        g = g_ref[:, h * DV_R:(h + 1) * DV_R]
        r_ref[:, h * DV_R:(h + 1) * DV_R] = (_head_rms(o) * (g * jax.nn.sigmoid(g))).astype(_BF16)

    @pl.when(c == pl.num_programs(1) - 1)
    def _():
        s_out_ref[...] = s_ref[...]


def _retention_prompt(qr, kr, vr, gr, batch, seq):
    ck = RET_CHUNK
    nc = seq // ck
    lg = _retention_log_decay()
    i = jnp.arange(ck, dtype=_F32)
    d = i[:, None] - i[None, :]
    dmat = jnp.where(d >= 0, jnp.exp(d[None] * lg[:, None, None]), 0.0)
    rowdec = jnp.broadcast_to(jnp.exp((i + 1.0)[None, :] * lg[:, None])[:, :, None], (H_R, ck, DV_R))
    kdec = jnp.broadcast_to(jnp.exp((ck - 1.0 - i)[None, :] * lg[:, None])[:, :, None], (H_R, ck, DK_R))
    sdec = jnp.broadcast_to(jnp.exp(ck * lg)[:, None, None], (H_R, 1, DV_R))
    row = lambda b, c: (b * nc + c, 0)
    return pl.pallas_call(
        _ret_body,
        grid=(batch, nc),
        in_specs=[pl.BlockSpec((ck, WR_QK), row), pl.BlockSpec((ck, WR_QK), row),
                  pl.BlockSpec((ck, WR_V), row), pl.BlockSpec((ck, WR_V), row),
                  _const_spec((H_R, ck, ck)), _const_spec((H_R, ck, DV_R)),
                  _const_spec((H_R, ck, DK_R)), _const_spec((H_R, 1, DV_R))],
        out_specs=[pl.BlockSpec((ck, WR_V), row),
                   pl.BlockSpec((None, H_R, DK_R, DV_R), lambda b, c: (b, 0, 0, 0))],
        out_shape=[jax.ShapeDtypeStruct((batch * seq, WR_V), _BF16),
                   jax.ShapeDtypeStruct((batch, H_R, DK_R, DV_R), _F32)],
        scratch_shapes=[pltpu.VMEM((H_R, DK_R, DV_R), _F32)],
        compiler_params=_params("parallel", "arbitrary"),
        name="retention_prompt",
    )(qr, kr, vr, gr, dmat, rowdec, kdec, sdec)


def _mix_out(x, a, r, w_out_ref):
    return x + _dot(a, w_out_ref[0:WA_V, :]) + _dot(r, w_out_ref[WA_V:, :])


def _mem_query(x1, g_ref, w_mq_ref):
    return _dot(_rmsnorm(x1, g_ref[...]).astype(_BF16), w_mq_ref[...]).astype(_BF16)


def _mem_attend_shared(q, mk_ref, mv_ref):
    outs = []
    for h in range(H_M):
        cols = slice(h * DH_M, (h + 1) * DH_M)
        s = _dot_nt(q[:, cols], mk_ref[:, cols].astype(_BF16)) * (DH_M ** -0.5)
        p = jnp.exp(s - jnp.max(s, axis=-1, keepdims=True))
        p = p / jnp.sum(p, axis=-1, keepdims=True)
        outs.append(_dot(p.astype(_BF16), mv_ref[:, cols].astype(_BF16)).astype(_BF16))
    return jnp.concatenate(outs, axis=-1)


def _tail(x1, o, w_mo_ref, g_mlp_ref, w_up_ref, w_down_ref, g_fin_ref):
    x2 = x1 + _dot(o, w_mo_ref[...])
    hm = _rmsnorm(x2, g_mlp_ref[...]).astype(_BF16)
    acc = x2
    for j in range(D_FF // FF_CHUNK):
        u = jnp.maximum(_dot(hm, w_up_ref[:, j * FF_CHUNK:(j + 1) * FF_CHUNK]), 0.0)
        acc = acc + _dot((u * u).astype(_BF16), w_down_ref[j * FF_CHUNK:(j + 1) * FF_CHUNK, :])
    return _rmsnorm(acc, g_fin_ref[...])


def _post_prompt_body(x_ref, a_ref, r_ref, mk_ref, mv_ref, w_out_ref, g_mq_ref, w_mq_ref, w_mo_ref,
                      g_mlp_ref, w_up_ref, w_down_ref, g_fin_ref, y_ref):
    x1 = _mix_out(x_ref[...], a_ref[...], r_ref[...], w_out_ref)
    o = _mem_attend_shared(_mem_query(x1, g_mq_ref, w_mq_ref), mk_ref, mv_ref)
    y_ref[...] = _tail(x1, o, w_mo_ref, g_mlp_ref, w_up_ref, w_down_ref, g_fin_ref)


def _pre_sample_body(x_ref, a_ref, r_ref, w_out_ref, g_mq_ref, w_mq_ref, x1_ref, q_ref):
    x1 = _mix_out(x_ref[...], a_ref[...], r_ref[...], w_out_ref)
    x1_ref[...] = x1
    q_ref[...] = _mem_query(x1, g_mq_ref, w_mq_ref)


def _tail_sample_body(x1_ref, o_ref, w_mo_ref, g_mlp_ref, w_up_ref, w_down_ref, g_fin_ref, y_ref):
    y_ref[...] = _tail(x1_ref[...], o_ref[...], w_mo_ref, g_mlp_ref, w_up_ref, w_down_ref, g_fin_ref)


def _post_prompt(x2d, a, r, mkv, wts, batch, seq):
    tm = TM_PROJ
    m = x2d.shape[0]
    nt = seq // tm
    row = lambda i: (i, 0)
    (w_out, g_mq, w_mq, w_mo, g_mlp, w_up, w_down, g_fin) = wts
    return pl.pallas_call(
        _post_prompt_body,
        grid=(m // tm,),
        in_specs=[pl.BlockSpec((tm, D_MODEL), row), pl.BlockSpec((tm, WA_V), row), pl.BlockSpec((tm, WR_V), row),
                  pl.BlockSpec((N_MEM, D_MODEL), lambda i: (i // nt, 0)),
                  pl.BlockSpec((N_MEM, D_MODEL), lambda i: (i // nt, 1)),
                  _const_spec(w_out.shape), _const_spec(g_mq.shape), _const_spec(w_mq.shape),
                  _const_spec(w_mo.shape), _const_spec(g_mlp.shape), _const_spec(w_up.shape),
                  _const_spec(w_down.shape), _const_spec(g_fin.shape)],
        out_specs=pl.BlockSpec((tm, D_MODEL), row),
        out_shape=jax.ShapeDtypeStruct((m, D_MODEL), _F32),
        compiler_params=_params("parallel"),
        name="post_prompt",
    )(x2d, a, r, mkv, mkv, *wts)


def _pre_sample(x2d, a, r, w_out, g_mq, w_mq):
    m = x2d.shape[0]
    full = lambda shape: pl.BlockSpec(shape, lambda i: (0,) * len(shape))
    return pl.pallas_call(
        _pre_sample_body,
        grid=(1,),
        in_specs=[full(x2d.shape), full(a.shape), full(r.shape), full(w_out.shape), full(g_mq.shape),
                  full(w_mq.shape)],
        out_specs=[full((m, D_MODEL)), full((m, D_MODEL))],
        out_shape=[jax.ShapeDtypeStruct((m, D_MODEL), _F32), jax.ShapeDtypeStruct((m, D_MODEL), _BF16)],
        compiler_params=_params("arbitrary"),
        name="pre_sample",
    )(x2d, a, r, w_out, g_mq, w_mq)


def _tail_sample(x1, o, w_mo, g_mlp, w_up, w_down, g_fin):
    m = x1.shape[0]
    full = lambda shape: pl.BlockSpec(shape, lambda i: (0,) * len(shape))
    args = (x1, o, w_mo, g_mlp, w_up, w_down, g_fin)
    return pl.pallas_call(
        _tail_sample_body,
        grid=(1,),
        in_specs=[full(t.shape) for t in args],
        out_specs=full((m, D_MODEL)),
        out_shape=jax.ShapeDtypeStruct((m, D_MODEL), _F32),
        compiler_params=_params("arbitrary"),
        name="tail_sample",
    )(*args)


def _memkv_body(x_ref, g_ref, w_ref, o_ref):
    o_ref[...] = _dot(_rmsnorm(x_ref[...], g_ref[...]).astype(_BF16), w_ref[...])


def _memkv(mem2d, g, w_bf):
    m, n = mem2d.shape[0], w_bf.shape[1]
    tm = N_MEM
    return pl.pallas_call(
        _memkv_body,
        grid=(m // tm,),
        in_specs=[pl.BlockSpec((tm, D_MODEL), lambda i: (i, 0)), _const_spec(g.shape), _const_spec(w_bf.shape)],
        out_specs=pl.BlockSpec((tm, n), lambda i: (i, 0)),
        out_shape=jax.ShapeDtypeStruct((m, n), _F32),
        compiler_params=_params("parallel"),
        name="memkv",
    )(mem2d, g, w_bf)


def _dattn_sample_body(pt_ref, lq1_ref, lk1_ref, lq2_ref, lk2_ref, subln_ref, bias_ref, bias_new_ref,
                       q_ref, knew_ref, vnew_ref, *rest, n_pages, lam_init):
    k_pages = rest[:n_pages]
    v_pages = rest[n_pages:2 * n_pages]
    o_ref = rest[2 * n_pages]
    nrow = 2 * H_A
    width = H_A * 2 * DK_A
    q = jnp.broadcast_to(q_ref[...].astype(_F32) * (DK_A ** -0.5), (nrow, width))
    row = lax.broadcasted_iota(jnp.int32, (nrow, width), 0)
    lane = lax.broadcasted_iota(jnp.int32, (nrow, width), 1)
    seg = (lane >> 7) + H_A * ((lane >> 6) & 1)
    qbd_f = jnp.where(seg == row, q, 0.0)
    qbd = qbd_f.astype(_BF16)

    s = jnp.concatenate([_dot_nt(qbd, kp[...].astype(_BF16)) for kp in k_pages], axis=-1) + bias_ref[...]
    knew = jnp.broadcast_to(knew_ref[...].astype(_BF16).astype(_F32), (nrow, width))
    s_new = jnp.sum(qbd_f * knew, axis=-1, keepdims=True) + bias_new_ref[...]
    m = jnp.maximum(jnp.max(s, axis=-1, keepdims=True), s_new)
    p = jnp.exp(s - m)
    p_new = jnp.exp(s_new - m)
    inv_l = 1.0 / (jnp.sum(p, axis=-1, keepdims=True) + p_new)
    lam = _lambda_value(lq1_ref, lk1_ref, lq2_ref, lk2_ref, lam_init)
    p = p * inv_l
    p_new = p_new * inv_l
    attn = p[:H_A] - lam * p[H_A:]
    attn_new = p_new[:H_A] - lam * p_new[H_A:]
    attn8 = jnp.concatenate([attn, jnp.zeros_like(attn)], axis=0).astype(_BF16)
    vnew = vnew_ref[...].astype(_BF16).astype(_F32)
    res = attn_new.astype(_BF16).astype(_F32) * vnew
    acc = jnp.zeros((nrow, H_A * DV_A), _F32)
    for j, vp in enumerate(v_pages):
        acc = acc + _dot(attn8[:, j * PAGE_SIZE:(j + 1) * PAGE_SIZE], vp[...].astype(_BF16))
    res = res + acc[:H_A]
    o = jnp.concatenate([res[h:h + 1, h * DV_A:(h + 1) * DV_A] for h in range(H_A)], axis=0)
    o = _head_rms(o) * subln_ref[...] * (1.0 - lam_init)
    o_ref[...] = jnp.concatenate([o[h:h + 1] for h in range(H_A)], axis=-1).astype(_BF16)


def _dattn_sample(qa, ka_new, va_new, cache_k, cache_v, page_table, rel_bias, lams, subln, lam_init):
    bd, n_pages = page_table.shape
    past = n_pages * PAGE_SIZE
    width = H_A * 2 * DK_A
    bucket = _t5_bucket_np(past - np.arange(past + 1))
    bias_h = rel_bias[bucket].T.astype(_F32)
    bias8 = jnp.concatenate([bias_h, bias_h], axis=0)
    kc = cache_k.reshape(cache_k.shape[0], PAGE_SIZE, width)
    vc = cache_v.reshape(cache_v.shape[0], PAGE_SIZE, H_A * DV_A)
    vec = pl.BlockSpec((1, DK_A), lambda b, pt: (0, 0))
    row3 = lambda n: pl.BlockSpec((None, 1, n), lambda b, pt: (b, 0, 0))

    def page(j):
        return pl.BlockSpec((None, PAGE_SIZE, width), lambda b, pt: (pt[b, j], 0, 0))

    grid_spec = pltpu.PrefetchScalarGridSpec(
        num_scalar_prefetch=1,
        grid=(bd,),
        in_specs=[vec, vec, vec, vec, pl.BlockSpec((1, DV_A), lambda b, pt: (0, 0)),
                  pl.BlockSpec((2 * H_A, past), lambda b, pt: (0, 0)),
                  pl.BlockSpec((2 * H_A, 1), lambda b, pt: (0, 0)),
                  row3(width), row3(width), row3(H_A * DV_A)]
                 + [page(j) for j in range(n_pages)] + [page(j) for j in range(n_pages)],
        out_specs=row3(H_A * DV_A),
    )
    out = pl.pallas_call(
        functools.partial(_dattn_sample_body, n_pages=n_pages, lam_init=lam_init),
        grid_spec=grid_spec,
        out_shape=jax.ShapeDtypeStruct((bd, 1, H_A * DV_A), _BF16),
        compiler_params=_params("parallel"),
        name="dattn_sample",
    )(page_table, *lams, subln, bias8[:, :past], bias8[:, past:],
      qa.reshape(bd, 1, width), ka_new.reshape(bd, 1, width), va_new.reshape(bd, 1, H_A * DV_A),
      *([kc] * n_pages), *([vc] * n_pages))
    return out.reshape(bd, H_A * DV_A)


def _ret_sample_body(q_ref, k_ref, v_ref, g_ref, s_ref, r_ref, s_out_ref, *, bb):
    nrow = H_R * DK_R
    q = q_ref[...]
    k = k_ref[...]
    for h in range(H_R):
        gamma = 1.0 - 2.0 ** (-5.0 - h)
        vh = v_ref[:, h * DV_R:(h + 1) * DV_R].astype(_F32)
        acc = jnp.zeros((bb, DV_R), _F32)
        for d in range(DK_R):
            hd = h * DK_R + d
            srow = s_ref[pl.ds(hd, bb, stride=nrow), :]
            acc = acc + q[:, hd:hd + 1] * srow
            s_out_ref[pl.ds(hd, bb, stride=nrow), :] = gamma * srow + k[:, hd:hd + 1] * vh
        qk = jnp.sum(q[:, h * DK_R:(h + 1) * DK_R] * k[:, h * DK_R:(h + 1) * DK_R], axis=-1, keepdims=True)
        o = qk * vh + acc * gamma
        g = g_ref[:, h * DV_R:(h + 1) * DV_R]
        r_ref[:, h * DV_R:(h + 1) * DV_R] = (_head_rms(o) * (g * jax.nn.sigmoid(g))).astype(_BF16)


def _retention_sample(qr, kr, vr, gr, state):
    bd = qr.shape[0]
    bb = BB_RET
    nrow = H_R * DK_R
    s2d = state.reshape(bd * nrow, DV_R)
    row = lambda i: (i, 0)
    r, s_new = pl.pallas_call(
        functools.partial(_ret_sample_body, bb=bb),
        grid=(bd // bb,),
        in_specs=[pl.BlockSpec((bb, WR_QK), row), pl.BlockSpec((bb, WR_QK), row), pl.BlockSpec((bb, WR_V), row),
                  pl.BlockSpec((bb, WR_V), row), pl.BlockSpec((bb * nrow, DV_R), row)],
        out_specs=[pl.BlockSpec((bb, WR_V), row), pl.BlockSpec((bb * nrow, DV_R), row)],
        out_shape=[jax.ShapeDtypeStruct((bd, WR_V), _BF16), jax.ShapeDtypeStruct((bd * nrow, DV_R), _F32)],
        compiler_params=_params("parallel"),
        name="retention_sample",
    )(qr, kr, vr, gr, s2d)
    return r, s_new.reshape(bd, H_R, DK_R, DV_R)


def _mem_sample_body(q_ref, mk_ref, mv_ref, o_ref):
    nrow = 8
    q = jnp.broadcast_to(q_ref[...].astype(_F32), (nrow, D_MODEL))
    row = lax.broadcasted_iota(jnp.int32, (nrow, D_MODEL), 0)
    lane = lax.broadcasted_iota(jnp.int32, (nrow, D_MODEL), 1)
    qbd = jnp.where((lane >> 8) == row, q, 0.0).astype(_BF16)
    s = _dot_nt(qbd, mk_ref[...].astype(_BF16)) * (DH_M ** -0.5)
    p = jnp.exp(s - jnp.max(s, axis=-1, keepdims=True))
    p = p / jnp.sum(p, axis=-1, keepdims=True)
    res = _dot(p.astype(_BF16), mv_ref[...].astype(_BF16))
    o_ref[...] = jnp.concatenate([res[h:h + 1, h * DH_M:(h + 1) * DH_M] for h in range(H_M)],
                                 axis=-1).astype(_BF16)


def _mem_sample(q, mk, mv):
    bd = q.shape[0]
    row3 = pl.BlockSpec((None, 1, D_MODEL), lambda b: (b, 0, 0))
    kv = pl.BlockSpec((None, N_MEM, D_MODEL), lambda b: (b, 0, 0))
    out = pl.pallas_call(
        _mem_sample_body,
        grid=(bd,),
        in_specs=[row3, kv, kv],
        out_specs=row3,
        out_shape=jax.ShapeDtypeStruct((bd, 1, D_MODEL), _BF16),
        compiler_params=_params("parallel"),
        name="mem_sample",
    )(q.reshape(bd, 1, D_MODEL), mk.reshape(bd, N_MEM, D_MODEL), mv.reshape(bd, N_MEM, D_MODEL))
    return out.reshape(bd, D_MODEL)


def kernel(x_prompt, x_sample, mem_prompt, cache_k, cache_v, page_table, state_ret, cache_mem_k, cache_mem_v,
           rel_bias, norm_mix, w_in, lambda_q1, lambda_k1, lambda_q2, lambda_k2, subln_a, w_out, norm_mem_q,
           norm_mem_kv, w_mq, w_mk, w_mv, w_mo, norm_mlp, w_up, w_down, norm_final):
    batch, seq = x_prompt.shape[0], x_prompt.shape[1]
    bd = x_sample.shape[0]
    depth = w_in.shape[0]
    assert depth == 1 and x_sample.shape[1] == 1
    past = page_table.shape[1] * PAGE_SIZE
    n_seq_pages = seq // PAGE_SIZE
    layer = 0
    lam_init = 0.8 - 0.6 * math.exp(-0.3 * layer)

    vec = lambda t: t[layer].reshape(1, -1).astype(_F32)
    bf = lambda t: t[layer].astype(_BF16)
    g_mix, g_mq, g_mkv, g_mlp = vec(norm_mix), vec(norm_mem_q), vec(norm_mem_kv), vec(norm_mlp)
    g_fin = norm_final.reshape(1, -1).astype(_F32)
    lams = (vec(lambda_q1), vec(lambda_k1), vec(lambda_q2), vec(lambda_k2))
    subln = vec(subln_a)
    w_in_b, w_out_b, w_mq_b, w_mo_b, w_up_b, w_down_b = (bf(w_in), bf(w_out), bf(w_mq), bf(w_mo), bf(w_up),
                                                          bf(w_down))
    w_mkv_b = jnp.concatenate([bf(w_mk), bf(w_mv)], axis=-1)
    tail_w = (w_mo_b, g_mlp, w_up_b, w_down_b, g_fin)

    xp = x_prompt.reshape(batch * seq, D_MODEL)
    cos_p, sin_p = _rotary_tables(jnp.arange(seq))
    qa, kaf, kab, vaf, vab, qr, kr, vr, gr = _inproj(xp, g_mix, w_in_b, cos_p, sin_p, TM_PROJ)
    a = _dattn_prompt(qa, kab, vab, rel_bias, lams, subln, batch, seq, lam_init)
    r, s_p = _retention_prompt(qr, kr, vr, gr, batch, seq)
    mkv = _memkv(mem_prompt.reshape(batch * N_MEM, D_MODEL), g_mkv, w_mkv_b)
    y_p = _post_prompt(xp, a, r, mkv, (w_out_b, g_mq, w_mq_b) + tail_w, batch, seq)

    xd = x_sample.reshape(bd, D_MODEL)
    cos_d, sin_d = _rotary_tables(jnp.full((bd,), past, jnp.int32))
    qa_d, kaf_d, _, vaf_d, _, qr_d, kr_d, vr_d, gr_d = _inproj(xd, g_mix, w_in_b, cos_d, sin_d, bd)
    a_d = _dattn_sample(qa_d, kaf_d, vaf_d, cache_k[layer], cache_v[layer], page_table, rel_bias, lams, subln,
                        lam_init)
    r_d, s_d = _retention_sample(qr_d, kr_d, vr_d, gr_d, state_ret[layer])
    x1_d, qm_d = _pre_sample(xd, a_d, r_d, w_out_b, g_mq, w_mq_b)
    o_d = _mem_sample(qm_d, cache_mem_k[layer], cache_mem_v[layer])
    y_d = _tail_sample(x1_d, o_d, *tail_w)

    mkv4 = mkv.reshape(batch, N_MEM, 2, H_M, DH_M)
    return (y_p.reshape(batch, seq, D_MODEL),
            y_d.reshape(bd, 1, D_MODEL),
            kaf.reshape(1, batch, n_seq_pages, PAGE_SIZE, H_A, 2 * DK_A),
            vaf.reshape(1, batch, n_seq_pages, PAGE_SIZE, H_A, DV_A),
            s_p[None],
            mkv4[:, :, 0][None],
            mkv4[:, :, 1][None],
            kaf_d.reshape(1, bd, 1, H_A, 2 * DK_A),
            vaf_d.reshape(1, bd, 1, H_A, DV_A),
            s_d[None])
```
